```python
import math
import jax
import jax.numpy as jnp
from jax import lax
import numpy as np

D_MODEL = 1024
BATCH = 8
SEQ = 2048
DEPTH = 1
DEC_BATCH = 32
DEC_SEQ = 8
PAST_LEN = 8192
PAGE_SIZE = 128

ATTN_HEADS = 4
QK_DIM = 64
V_DIM = 2 * QK_DIM
ATTN_WIDTH = ATTN_HEADS * V_DIM
QK_COLS = ATTN_HEADS * 2 * QK_DIM
ROT_DIM = QK_DIM // 4
ROPE_THETA = 500000.0
Q_BLOCK = 128
RG_WIDTH = D_MODEL - ATTN_WIDTH
RG_BLOCKS = 8
RG_BLOCK_W = RG_WIDTH // RG_BLOCKS
RG_C = 8.0
CONV_WIDTH = 4
MIX_WIDTH = ATTN_WIDTH + RG_WIDTH
IN_WIDTH = 2 * QK_COLS + ATTN_WIDTH + 2 * RG_WIDTH
PEER_HEADS = 8
PEER_KEYS = 128
PEER_EXPERTS = PEER_KEYS * PEER_KEYS
PEER_TOPK = 16
PEER_QHALF = 128
PEER_BLOCK = 128
PLE_DIM = 256
LN_EPS = 1e-5
DEEPNORM_ALPHA = (2.0 * DEPTH) ** 0.25
DEEPNORM_BETA = (8.0 * DEPTH) ** -0.25

kernel_name = 'hybrid_diffattn_rglru_peer_step'


def layer_norm(x, g, b):
    xf = x.astype(jnp.float32)
    mu = jnp.mean(xf, -1, keepdims=True)
    var = jnp.mean(jnp.square(xf - mu), -1, keepdims=True)
    return ((xf - mu) * lax.rsqrt(var + LN_EPS)).astype(x.dtype) * g + b


def rotary(x, pos):
    half = ROT_DIM // 2
    inv_freq = jnp.float32(ROPE_THETA) ** (-jnp.arange(0, ROT_DIM, 2, dtype=jnp.float32) / ROT_DIM)
    ang = pos.astype(jnp.float32)[:, None] * inv_freq[None, :]
    cos = jnp.cos(ang)[:, None, None, :]
    sin = jnp.sin(ang)[:, None, None, :]
    xr = x[..., :ROT_DIM].astype(jnp.float32)
    x1, x2 = xr[..., :half], xr[..., half:]
    rot = jnp.concatenate([x1 * cos - x2 * sin, x2 * cos + x1 * sin], -1).astype(x.dtype)
    return jnp.concatenate([rot, x[..., ROT_DIM:]], -1)


def diff_attention(q, k, v, q_pos, k_pos, lam):
    s = jnp.einsum('bqhmd,bkhmd->bhmqk', q, k).astype(jnp.float32) * (QK_DIM ** -0.5)
    mask = k_pos[None, :] <= q_pos[:, None]
    p = jax.nn.softmax(jnp.where(mask, s, -jnp.inf), axis=-1)
    w = (p[:, :, 0] - lam * p[:, :, 1]).astype(v.dtype)
    return jnp.einsum('bhqk,bkhd->bqhd', w, v)


def prompt_attention(q, k, v, pos, lam):
    b, s = q.shape[:2]
    nb = s // Q_BLOCK
    qb = jnp.moveaxis(q.reshape(b, nb, Q_BLOCK, ATTN_HEADS, 2, QK_DIM), 1, 0)
    pb = pos.reshape(nb, Q_BLOCK)
    o = lax.map(lambda a: diff_attention(a[0], k, v, a[1], pos, lam), (qb, pb))
    return jnp.moveaxis(o, 0, 1).reshape(b, s, ATTN_HEADS, V_DIM)


def sample_attention(q, k, v, pos, past_k, past_v, lam):
    k_all = jnp.concatenate([past_k.astype(k.dtype), k], 1)
    v_all = jnp.concatenate([past_v.astype(v.dtype), v], 1)
    k_pos = jnp.arange(k_all.shape[1])
    return diff_attention(q, k_all, v_all, pos, k_pos, lam)


def _lin_combine(e1, e2):
    a1, b1 = e1
    a2, b2 = e2
    return a1 * a2, a2 * b1 + b2


def rg_lru_branch(xr, gate, h0, conv_buf, conv_w, conv_b, wa, ba, wx, bx, lam):
    b, s, c = xr.shape
    xp = jnp.concatenate([conv_buf.astype(xr.dtype), xr], 1)
    xc = conv_b + sum(xp[:, j:j + s] * conv_w[j] for j in range(CONV_WIDTH))
    conv_new = xp[:, -(CONV_WIDTH - 1):]
    xb = xc.reshape(b, s, RG_BLOCKS, RG_BLOCK_W)
    r = jax.nn.sigmoid(jnp.einsum('bsnc,ncd->bsnd', xb, wa).reshape(b, s, c) + ba)
    i = jax.nn.sigmoid(jnp.einsum('bsnc,ncd->bsnd', xb, wx).reshape(b, s, c) + bx)
    log_a = -RG_C * jax.nn.softplus(-lam.astype(jnp.float32)) * r.astype(jnp.float32)
    a = jnp.exp(log_a)
    u = jnp.sqrt(-jnp.expm1(2.0 * log_a)) * (i * xc).astype(jnp.float32)
    u = u.at[:, 0].add(a[:, 0] * h0.astype(jnp.float32))
    _, h = lax.associative_scan(_lin_combine, (a, u), axis=1)
    y = h.astype(xr.dtype) * jax.nn.gelu(gate)
    return y, h[:, -1].astype(xr.dtype), conv_new


def peer_ffn(x, wq, keys, u_tab, v_tab):
    b, s, d = x.shape
    t = x.reshape(b * s, d)
    n_tok = t.shape[0]
    n_blk = -(-n_tok // PEER_BLOCK)
    t = jnp.pad(t, ((0, n_blk * PEER_BLOCK - n_tok), (0, 0))).reshape(n_blk, PEER_BLOCK, d)

    def block(tb):
        q = (tb @ wq).reshape(PEER_BLOCK, PEER_HEADS, 2, PEER_QHALF)
        sc = jnp.einsum('thpd,hpnd->thpn', q, keys).astype(jnp.float32)
        s1, i1 = lax.top_k(sc[:, :, 0], PEER_TOPK)
        s2, i2 = lax.top_k(sc[:, :, 1], PEER_TOPK)
        cand = (s1[..., :, None] + s2[..., None, :]).reshape(PEER_BLOCK, PEER_HEADS, PEER_TOPK * PEER_TOPK)
        cidx = (i1[..., :, None] * PEER_KEYS + i2[..., None, :]).reshape(PEER_BLOCK, PEER_HEADS, PEER_TOPK * PEER_TOPK)
        top_s, top_p = lax.top_k(cand, PEER_TOPK)
        eidx = jnp.take_along_axis(cidx, top_p, axis=-1)
        g = jax.nn.softmax(top_s, axis=-1)
        act = jax.nn.gelu(jnp.einsum('td,thkd->thk', tb, u_tab[eidx]))
        w = (g * act.astype(jnp.float32)).astype(tb.dtype)
        return jnp.einsum('thk,thkd->td', w, v_tab[eidx])

    out = lax.map(block, t).reshape(n_blk * PEER_BLOCK, d)[:n_tok]
    return out.reshape(b, s, d)


def decoder_layer(x, pe, pos, h0, conv_buf, attend, lam_init, w):
    b, s, _ = x.shape
    proj = x @ w['w_in']
    c1 = QK_COLS
    c2 = 2 * QK_COLS
    c3 = c2 + ATTN_WIDTH
    c4 = c3 + RG_WIDTH
    q, k, v, xr, gate = proj[..., :c1], proj[..., c1:c2], proj[..., c2:c3], proj[..., c3:c4], proj[..., c4:]
    q = rotary(q.reshape(b, s, ATTN_HEADS, 2, QK_DIM), pos)
    k = rotary(k.reshape(b, s, ATTN_HEADS, 2, QK_DIM), pos)
    v = v.reshape(b, s, ATTN_HEADS, V_DIM)
    o = attend(q, k, v).astype(jnp.float32)
    o = o * lax.rsqrt(jnp.mean(o * o, -1, keepdims=True) + LN_EPS) * (1.0 - lam_init)
    o = (o.astype(x.dtype) * w['subln_g']).reshape(b, s, ATTN_WIDTH)
    yr, h_last, conv_new = rg_lru_branch(xr, gate, h0, conv_buf, w['conv_w'], w['conv_b'],
                                         w['rg_wa'], w['rg_ba'], w['rg_wx'], w['rg_bx'], w['rg_lambda'])
    mix = jnp.concatenate([o, yr], -1) @ w['w_out']
    x1 = layer_norm(DEEPNORM_ALPHA * x + mix, w['ln1_g'], w['ln1_b'])
    ffn = peer_ffn(x1, w['peer_wq'], w['peer_keys'], w['peer_u'], w['peer_v'])
    x2 = layer_norm(DEEPNORM_ALPHA * x1 + ffn, w['ln2_g'], w['ln2_b'])
    gate_ple = jax.nn.sigmoid(x2 @ w['ple_gate_w'] + w['ple_gate_b'])
    y = x2 + gate_ple * (pe.astype(x.dtype) @ w['ple_w'])
    return y, k.reshape(b, s, ATTN_HEADS, 2 * QK_DIM), v, h_last, conv_new


def setup_inputs(seed: int = 0) -> dict:
    key = jax.random.key(seed)
    ks = iter(jax.random.split(key, 40))
    n_pages = PAST_LEN // PAGE_SIZE
    n_pool = (DEC_BATCH * n_pages * 5) // 4

    def nrm(shape, scale):
        return jax.random.normal(next(ks), shape, jnp.float32) * scale

    x_prompt = nrm((BATCH, SEQ, D_MODEL), 1.0)
    x_sample = nrm((DEC_BATCH, DEC_SEQ, D_MODEL), 1.0)
    p_prompt = nrm((DEPTH, BATCH, SEQ, PLE_DIM), 1.0)
    p_sample = nrm((DEPTH, DEC_BATCH, DEC_SEQ, PLE_DIM), 1.0)
    cache_k = nrm((DEPTH, n_pool, PAGE_SIZE, ATTN_HEADS, 2 * QK_DIM), 1.0)
    cache_v = nrm((DEPTH, n_pool, PAGE_SIZE, ATTN_HEADS, V_DIM), 1.0)
    state_h = nrm((DEPTH, DEC_BATCH, RG_WIDTH), 0.5)
    state_conv = nrm((DEPTH, DEC_BATCH, CONV_WIDTH - 1, RG_WIDTH), 1.0)
    page_table = jax.random.permutation(next(ks), n_pool)[:DEC_BATCH * n_pages].reshape(DEC_BATCH, n_pages).astype(jnp.int32)
    w_in = nrm((DEPTH, D_MODEL, IN_WIDTH), D_MODEL ** -0.5)
    lambda_q1 = nrm((DEPTH, QK_DIM), 0.1)
    lambda_k1 = nrm((DEPTH, QK_DIM), 0.1)
    lambda_q2 = nrm((DEPTH, QK_DIM), 0.1)
    lambda_k2 = nrm((DEPTH, QK_DIM), 0.1)
    subln_g = 1.0 + nrm((DEPTH, V_DIM), 0.01)
    conv_w = nrm((DEPTH, CONV_WIDTH, RG_WIDTH), CONV_WIDTH ** -0.5)
    conv_b = nrm((DEPTH, RG_WIDTH), 0.01)
    rg_wa = nrm((DEPTH, RG_BLOCKS, RG_BLOCK_W, RG_BLOCK_W), RG_BLOCK_W ** -0.5)
    rg_ba = nrm((DEPTH, RG_WIDTH), 0.01)
    rg_wx = nrm((DEPTH, RG_BLOCKS, RG_BLOCK_W, RG_BLOCK_W), RG_BLOCK_W ** -0.5)
    rg_bx = nrm((DEPTH, RG_WIDTH), 0.01)
    a_pow = jax.random.uniform(next(ks), (DEPTH, RG_WIDTH), jnp.float32, 0.9, 0.999)
    sig = a_pow ** (1.0 / RG_C)
    rg_lambda = jnp.log(sig) - jnp.log1p(-sig)
    w_out = nrm((DEPTH, MIX_WIDTH, D_MODEL), DEEPNORM_BETA * MIX_WIDTH ** -0.5)
    ln1_g = 1.0 + nrm((DEPTH, D_MODEL), 0.01)
    ln1_b = nrm((DEPTH, D_MODEL), 0.01)
    peer_wq = nrm((DEPTH, D_MODEL, PEER_HEADS * 2 * PEER_QHALF), D_MODEL ** -0.5)
    peer_keys = nrm((DEPTH, PEER_HEADS, 2, PEER_KEYS, PEER_QHALF), PEER_QHALF ** -0.5)
    peer_u = nrm((DEPTH, PEER_EXPERTS, D_MODEL), D_MODEL ** -0.5)
    peer_v = nrm((DEPTH, PEER_EXPERTS, D_MODEL), DEEPNORM_BETA * PEER_HEADS ** -0.5)
    ln2_g = 1.0 + nrm((DEPTH, D_MODEL), 0.01)
    ln2_b = nrm((DEPTH, D_MODEL), 0.01)
    ple_w = nrm((DEPTH, PLE_DIM, D_MODEL), DEEPNORM_BETA * PLE_DIM ** -0.5)
    ple_gate_w = nrm((DEPTH, D_MODEL, D_MODEL), D_MODEL ** -0.5)
    ple_gate_b = nrm((DEPTH, D_MODEL), 0.01)
    return {'x_prompt': x_prompt, 'x_sample': x_sample, 'p_prompt': p_prompt, 'p_sample': p_sample,
            'cache_k': cache_k, 'cache_v': cache_v, 'state_h': state_h, 'state_conv': state_conv,
            'page_table': page_table, 'w_in': w_in, 'lambda_q1': lambda_q1, 'lambda_k1': lambda_k1,
            'lambda_q2': lambda_q2, 'lambda_k2': lambda_k2, 'subln_g': subln_g, 'conv_w': conv_w,
            'conv_b': conv_b, 'rg_wa': rg_wa, 'rg_ba': rg_ba, 'rg_wx': rg_wx, 'rg_bx': rg_bx,
            'rg_lambda': rg_lambda, 'w_out': w_out, 'ln1_g': ln1_g, 'ln1_b': ln1_b,
            'peer_wq': peer_wq, 'peer_keys': peer_keys, 'peer_u': peer_u, 'peer_v': peer_v,
            'ln2_g': ln2_g, 'ln2_b': ln2_b, 'ple_w': ple_w, 'ple_gate_w': ple_gate_w,
            'ple_gate_b': ple_gate_b}


def reference(x_prompt, x_sample, p_prompt, p_sample, cache_k, cache_v, state_h, state_conv,
              page_table, w_in, lambda_q1, lambda_k1, lambda_q2, lambda_k2, subln_g, conv_w,
              conv_b, rg_wa, rg_ba, rg_wx, rg_bx, rg_lambda, w_out, ln1_g, ln1_b, peer_wq,
              peer_keys, peer_u, peer_v, ln2_g, ln2_b, ple_w, ple_gate_w, ple_gate_b):
    bp, s = x_prompt.shape[:2]
    bd, t = x_sample.shape[:2]
    past = page_table.shape[1] * cache_k.shape[2]
    pos_p = jnp.arange(s)
    pos_d = past + jnp.arange(t)
    xp, xd = x_prompt, x_sample
    kp_l, vp_l, hp_l, cp_l, kd_l, vd_l, hd_l, cd_l = [], [], [], [], [], [], [], []
    for l in range(DEPTH):
        lam_init = 0.8 - 0.6 * math.exp(-0.3 * l)
        lam = (jnp.exp(jnp.sum(lambda_q1[l].astype(jnp.float32) * lambda_k1[l].astype(jnp.float32)))
               - jnp.exp(jnp.sum(lambda_q2[l].astype(jnp.float32) * lambda_k2[l].astype(jnp.float32)))
               + lam_init)
        w = {'w_in': w_in[l], 'subln_g': subln_g[l], 'conv_w': conv_w[l], 'conv_b': conv_b[l],
             'rg_wa': rg_wa[l], 'rg_ba': rg_ba[l], 'rg_wx': rg_wx[l], 'rg_bx': rg_bx[l],
             'rg_lambda': rg_lambda[l], 'w_out': w_out[l], 'ln1_g': ln1_g[l], 'ln1_b': ln1_b[l],
             'peer_wq': peer_wq[l], 'peer_keys': peer_keys[l], 'peer_u': peer_u[l],
             'peer_v': peer_v[l], 'ln2_g': ln2_g[l], 'ln2_b': ln2_b[l], 'ple_w': ple_w[l],
             'ple_gate_w': ple_gate_w[l], 'ple_gate_b': ple_gate_b[l]}
        h0_p = jnp.zeros((bp, RG_WIDTH), xp.dtype)
        cb_p = jnp.zeros((bp, CONV_WIDTH - 1, RG_WIDTH), xp.dtype)
        xp, kp, vp, hp, cp = decoder_layer(
            xp, p_prompt[l], pos_p, h0_p, cb_p,
            lambda q, k, v: prompt_attention(q, k, v, pos_p, lam), lam_init, w)
        past_k = cache_k[l][page_table].reshape(bd, past, ATTN_HEADS, 2, QK_DIM)
        past_v = cache_v[l][page_table].reshape(bd, past, ATTN_HEADS, V_DIM)
        xd, kd, vd, hd, cd = decoder_layer(
            xd, p_sample[l], pos_d, state_h[l], state_conv[l],
            lambda q, k, v: sample_attention(q, k, v, pos_d, past_k, past_v, lam), lam_init, w)
        kp_l.append(kp)
        vp_l.append(vp)
        hp_l.append(hp)
        cp_l.append(cp)
        kd_l.append(kd)
        vd_l.append(vd)
        hd_l.append(hd)
        cd_l.append(cd)
    k_prompt = jnp.stack(kp_l)
    v_prompt = jnp.stack(vp_l)
    h_prompt = jnp.stack(hp_l)
    conv_prompt = jnp.stack(cp_l)
    k_sample = jnp.stack(kd_l)
    v_sample = jnp.stack(vd_l)
    h_sample = jnp.stack(hd_l)
    conv_sample = jnp.stack(cd_l)
    return (xp, xd, k_prompt, v_prompt, h_prompt, conv_prompt, k_sample, v_sample, h_sample, conv_sample)
```

```python
import functools
import math

import jax
import jax.numpy as jnp
from jax import lax
from jax.experimental import pallas as pl
from jax.experimental.pallas import tpu as pltpu

F32 = jnp.float32
BF16 = jnp.bfloat16

LANES = 128
SUBLANES = 8
VMEM_LIMIT_BYTES = 56 * 1024 * 1024

ATTN_HEADS = 4
QK_DIM = 64
V_DIM = 2 * QK_DIM
ATTN_WIDTH = ATTN_HEADS * V_DIM
QK_COLS = ATTN_HEADS * 2 * QK_DIM
ROT_DIM = QK_DIM // 4
ROPE_THETA = 500000.0
RG_BLOCKS = 8
RG_C = 8.0
CONV_WIDTH = 4
PEER_HEADS = 8
PEER_KEYS = 128
PEER_TOPK = 16
PEER_QHALF = 128
LN_EPS = 1e-5
NEG_BIG = -1e30

PROJ_TM = 512
ATTN_TQ = 256
SATTN_PAGES = 16
RG_CHUNK = 256
TOK_TM = 512
ROUTE_TT = 256
DENSE_TM = 512
DENSE_I1 = 8
DENSE_ROWS = 64


def _cparams(sem):
    return pltpu.CompilerParams(dimension_semantics=sem, vmem_limit_bytes=VMEM_LIMIT_BYTES)


def _gelu_tanh(x):
    return 0.5 * x * (1.0 + jnp.tanh(0.7978845608028654 * (x + 0.044715 * (x * x * x))))


def _sigmoid(x):
    return 1.0 / (1.0 + jnp.exp(-x))


def _lambda_scalar(lq1, lk1, lq2, lk2, lam_init):
    a = jnp.exp(jnp.sum(lq1 * lk1, axis=1, keepdims=True))
    b = jnp.exp(jnp.sum(lq2 * lk2, axis=1, keepdims=True))
    return a - b + lam_init


def _subln(o, g, lam_init):
    ms = jnp.mean(o * o, axis=1, keepdims=True)
    return o * lax.rsqrt(ms + LN_EPS) * (1.0 - lam_init) * g


def _proj_kernel(x_ref, w_ref, cos_ref, sa_ref, sb_ref,
                 q_ref, kb_ref, vb_ref, k_ref, v_ref, xr_ref, gate_ref):
    x = x_ref[...].astype(BF16)
    cos = cos_ref[...]
    sa = sa_ref[...]
    sb = sb_ref[...]

    def rot(c):
        return c * cos + pltpu.roll(c, ROT_DIM // 2, 1) * sa + pltpu.roll(c, LANES - ROT_DIM // 2, 1) * sb

    qk = jnp.dot(x, w_ref[:, 0:2 * QK_COLS], preferred_element_type=F32)
    for j in range(QK_COLS // LANES):
        cols = slice(j * LANES, (j + 1) * LANES)
        q_ref[:, cols] = (rot(qk[:, cols]) * (QK_DIM ** -0.5)).astype(BF16)
        kcols = slice(QK_COLS + j * LANES, QK_COLS + (j + 1) * LANES)
        kr = rot(qk[:, kcols])
        k_ref[:, cols] = kr
        kb_ref[:, cols] = kr.astype(BF16)
    c0 = 2 * QK_COLS
    v = jnp.dot(x, w_ref[:, c0:c0 + ATTN_WIDTH], preferred_element_type=F32)
    v_ref[...] = v
    vb_ref[...] = v.astype(BF16)
    c1 = c0 + ATTN_WIDTH
    rgw = xr_ref.shape[-1]
    xr_ref[...] = jnp.dot(x, w_ref[:, c1:c1 + rgw], preferred_element_type=F32)
    gate_ref[...] = jnp.dot(x, w_ref[:, c1 + rgw:c1 + 2 * rgw], preferred_element_type=F32)


def _rotary_tables(pos):
    half = ROT_DIM // 2
    inv_freq = jnp.float32(ROPE_THETA) ** (-jnp.arange(0, ROT_DIM, 2, dtype=jnp.float32) / ROT_DIM)
    ang = pos.astype(jnp.float32)[:, None] * inv_freq[None, :]
    cos = jnp.cos(ang)
    sin = jnp.sin(ang)
    n = pos.shape[0]
    ones = jnp.ones((n, QK_DIM - ROT_DIM), F32)
    zeros_h = jnp.zeros((n, half), F32)
    zeros_r = jnp.zeros((n, QK_DIM - ROT_DIM), F32)
    cos64 = jnp.concatenate([cos, cos, ones], axis=1)
    sa64 = jnp.concatenate([zeros_h, sin, zeros_r], axis=1)
    sb64 = jnp.concatenate([-sin, zeros_h, zeros_r], axis=1)
    rep = LANES // QK_DIM
    return (jnp.tile(cos64, (1, rep)), jnp.tile(sa64, (1, rep)), jnp.tile(sb64, (1, rep)))


def _project(x, w_in_b, tables, tm):
    b, s, d = x.shape
    rgw = (w_in_b.shape[1] - 2 * QK_COLS - ATTN_WIDTH) // 2
    row = lambda si, bi: (bi, si, 0)
    tab = lambda si, bi: (si, 0)
    out_shape = (
        jax.ShapeDtypeStruct((b, s, QK_COLS), BF16),
        jax.ShapeDtypeStruct((b, s, QK_COLS), BF16),
        jax.ShapeDtypeStruct((b, s, ATTN_WIDTH), BF16),
        jax.ShapeDtypeStruct((b, s, QK_COLS), F32),
        jax.ShapeDtypeStruct((b, s, ATTN_WIDTH), F32),
        jax.ShapeDtypeStruct((b, s, rgw), F32),
        jax.ShapeDtypeStruct((b, s, rgw), F32),
    )
    return pl.pallas_call(
        _proj_kernel,
        grid=(s // tm, b),
        in_specs=[
            pl.BlockSpec((None, tm, d), row),
            pl.BlockSpec(w_in_b.shape, lambda si, bi: (0, 0)),
            pl.BlockSpec((tm, LANES), tab),
            pl.BlockSpec((tm, LANES), tab),
            pl.BlockSpec((tm, LANES), tab),
        ],
        out_specs=tuple(pl.BlockSpec((None, tm, o.shape[-1]), row) for o in out_shape),
        out_shape=out_shape,
        compiler_params=_cparams(("parallel", "parallel")),
        name="proj_rotary",
    )(x, w_in_b, *tables)


def _pattn_kernel(lq1_ref, lk1_ref, lq2_ref, lk2_ref, g_ref, q_ref, k_ref, v_ref, o_ref, *, lam_init):
    tq = q_ref.shape[0]
    qi = pl.program_id(2)
    q = q_ref[...]
    col = lax.broadcasted_iota(jnp.int32, q.shape, 1)
    zero = jnp.zeros_like(q)
    q2 = jnp.concatenate([jnp.where(col < QK_DIM, q, zero), jnp.where(col >= QK_DIM, q, zero)], axis=0)

    def block(j, carry, masked):
        m, l, acc = carry
        start = pl.multiple_of(j * tq, tq)
        kb = k_ref[pl.ds(start, tq), :]
        vb = v_ref[pl.ds(start, tq), :]
        s = lax.dot_general(q2, kb, (((1,), (1,)), ((), ())), preferred_element_type=F32)
        if masked:
            r = lax.broadcasted_iota(jnp.int32, (2 * tq, tq), 0)
            c = lax.broadcasted_iota(jnp.int32, (2 * tq, tq), 1)
            s = jnp.where(c <= jnp.where(r >= tq, r - tq, r), s, -jnp.inf)
        m_new = jnp.maximum(m, jnp.max(s, axis=1, keepdims=True))
        alpha = jnp.exp(m - m_new)
        p = jnp.exp(s - m_new)
        l = alpha * l + jnp.sum(p, axis=1, keepdims=True)
        acc = alpha * acc + jnp.dot(p.astype(BF16), vb, preferred_element_type=F32)
        return m_new, l, acc

    init = (jnp.full((2 * tq, 1), NEG_BIG, F32), jnp.zeros((2 * tq, 1), F32), jnp.zeros((2 * tq, V_DIM), F32))
    carry = lax.fori_loop(0, qi, lambda j, c: block(j, c, False), init)
    m, l, acc = block(qi, carry, True)
    lam = _lambda_scalar(lq1_ref[...], lk1_ref[...], lq2_ref[...], lk2_ref[...], lam_init)
    o = acc[:tq] / l[:tq] - lam * (acc[tq:] / l[tq:])
    o_ref[...] = _subln(o, g_ref[...], lam_init).astype(o_ref.dtype)


def _prompt_attention(q, kb, vb, lam_vecs, g, lam_init):
    b, s, _ = q.shape
    tq = min(ATTN_TQ, s)
    vec = pl.BlockSpec((1, QK_DIM), lambda bi, h, qi: (0, 0))
    return pl.pallas_call(
        functools.partial(_pattn_kernel, lam_init=lam_init),
        grid=(b, ATTN_HEADS, s // tq),
        in_specs=[vec, vec, vec, vec,
                  pl.BlockSpec((1, V_DIM), lambda bi, h, qi: (0, 0)),
                  pl.BlockSpec((None, tq, V_DIM), lambda bi, h, qi: (bi, qi, h)),
                  pl.BlockSpec((None, s, V_DIM), lambda bi, h, qi: (bi, 0, h)),
                  pl.BlockSpec((None, s, V_DIM), lambda bi, h, qi: (bi, 0, h))],
        out_specs=pl.BlockSpec((None, tq, V_DIM), lambda bi, h, qi: (bi, qi, h)),
        out_shape=jax.ShapeDtypeStruct((b, s, ATTN_WIDTH), BF16),
        compiler_params=_cparams(("parallel", "parallel", "arbitrary")),
        name="prompt_attention",
    )(*lam_vecs, g, q, kb, vb)


def _sattn_kernel(pt_ref, lq1_ref, lk1_ref, lq2_ref, lk2_ref, g_ref, q_ref, kn_ref, vn_ref, *rest,
                  n_pages, lam_init):
    kp = rest[:n_pages]
    vp = rest[n_pages:2 * n_pages]
    o_ref = rest[2 * n_pages]
    qbd_s, m_s, l_s, acc_s = rest[2 * n_pages + 1:]
    t = q_ref.shape[0]
    t_shift = t.bit_length() - 1
    qk_shift = QK_DIM.bit_length() - 1
    n_maps = ATTN_HEADS * 2
    rows = n_maps * t
    c = pl.program_id(1)
    nt = (((1,), (1,)), ((), ()))

    @pl.when(c == 0)
    def _():
        q = q_ref[...]
        qt = jnp.concatenate([q] * n_maps, axis=0)
        r = lax.broadcasted_iota(jnp.int32, qt.shape, 0)
        col = lax.broadcasted_iota(jnp.int32, qt.shape, 1)
        qbd_s[...] = jnp.where(r >> t_shift == col >> qk_shift, qt, jnp.zeros_like(qt))
        m_s[...] = jnp.full(m_s.shape, NEG_BIG, F32)
        l_s[...] = jnp.zeros(l_s.shape, F32)
        acc_s[...] = jnp.zeros(acc_s.shape, F32)

    qbd = qbd_s[...]

    def update(s_list, v_list):
        m_prev = m_s[...]
        m_new = m_prev
        for s in s_list:
            m_new = jnp.maximum(m_new, jnp.max(s, axis=1, keepdims=True))
        alpha = jnp.exp(m_prev - m_new)
        l = alpha * l_s[...]
        pv = jnp.zeros(acc_s.shape, F32)
        for s, v in zip(s_list, v_list):
            p = jnp.exp(s - m_new)
            l = l + jnp.sum(p, axis=1, keepdims=True)
            pv = pv + jnp.dot(p.astype(BF16), v, preferred_element_type=F32)
        m_s[...] = m_new
        l_s[...] = l
        acc_s[...] = alpha * acc_s[...] + pv

    s_list = [lax.dot_general(qbd, kp[j][...].astype(BF16), nt, preferred_element_type=F32)
              for j in range(n_pages)]
    update(s_list, [vp[j][...].astype(BF16) for j in range(n_pages)])

    @pl.when(c == pl.num_programs(1) - 1)
    def _():
        pad = jnp.zeros((LANES - t, kn_ref.shape[1]), BF16)
        kn = jnp.concatenate([kn_ref[...], pad], axis=0)
        vn = jnp.concatenate([vn_ref[...], pad], axis=0)
        s = lax.dot_general(qbd, kn, nt, preferred_element_type=F32)
        r = lax.broadcasted_iota(jnp.int32, s.shape, 0)
        j = lax.broadcasted_iota(jnp.int32, s.shape, 1)
        s = jnp.where(j <= (r & (t - 1)), s, -jnp.inf)
        update([s], [vn])
        lam = _lambda_scalar(lq1_ref[...], lk1_ref[...], lq2_ref[...], lk2_ref[...], lam_init)
        acc = acc_s[...]
        l = l_s[...]
        for h in range(ATTN_HEADS):
            r1 = slice((2 * h) * t, (2 * h + 1) * t)
            r2 = slice((2 * h + 1) * t, (2 * h + 2) * t)
            cols = slice(h * V_DIM, (h + 1) * V_DIM)
            o = acc[r1, cols] / l[r1] - lam * (acc[r2, cols] / l[r2])
            o_ref[:, cols] = _subln(o, g_ref[...], lam_init).astype(o_ref.dtype)


def _sample_attention(q, kb, vb, cache_k2, cache_v2, page_table, page_base, lam_vecs, g, lam_init):
    b, t, _ = q.shape
    n_tab = page_table.shape[1]
    n_pages = min(SATTN_PAGES, n_tab)
    page_rows, width = cache_k2.shape[1], cache_k2.shape[2]
    rows = ATTN_HEADS * 2 * t
    vec = pl.BlockSpec((1, QK_DIM), lambda bi, c, pt: (0, 0))
    tok = pl.BlockSpec((None, t, width), lambda bi, c, pt: (bi, 0, 0))

    def page_spec(j):
        return pl.BlockSpec((None, page_rows, width),
                            lambda bi, c, pt: (pt[bi, c * n_pages + j] + page_base, 0, 0))

    grid_spec = pltpu.PrefetchScalarGridSpec(
        num_scalar_prefetch=1,
        grid=(b, n_tab // n_pages),
        in_specs=[vec, vec, vec, vec, pl.BlockSpec((1, V_DIM), lambda bi, c, pt: (0, 0)), tok, tok, tok]
                 + [page_spec(j) for j in range(n_pages)] + [page_spec(j) for j in range(n_pages)],
        out_specs=pl.BlockSpec((None, t, width), lambda bi, c, pt: (bi, 0, 0)),
        scratch_shapes=[pltpu.VMEM((rows, width), BF16), pltpu.VMEM((rows, 1), F32),
                        pltpu.VMEM((rows, 1), F32), pltpu.VMEM((rows, width), F32)],
    )
    return pl.pallas_call(
        functools.partial(_sattn_kernel, n_pages=n_pages, lam_init=lam_init),
        grid_spec=grid_spec,
        out_shape=jax.ShapeDtypeStruct((b, t, ATTN_WIDTH), BF16),
        compiler_params=_cparams(("parallel", "arbitrary")),
        name="sample_attention",
    )(page_table, *lam_vecs, g, q, kb, vb, *([cache_k2] * n_pages), *([cache_v2] * n_pages))


def _rglru_kernel(xr_ref, gate_ref, h0_ref, cbuf_ref, cw_ref, cb_ref, wa_ref, ba_ref, wx_ref, bx_ref,
                  lam_ref, y_ref, hlast_ref, cnew_ref, xp_s, a_s, u_s, h_s, *, chunk):
    s, c = xr_ref.shape
    pad = SUBLANES
    keep = CONV_WIDTH - 1
    xp_s[0:pad, :] = jnp.concatenate([jnp.zeros((pad - keep, c), F32), cbuf_ref[...]], axis=0)
    xp_s[pad:pad + s, :] = xr_ref[...]
    cnew_ref[...] = xp_s[pad + s - keep:pad + s, :]

    lam = lam_ref[...]
    neg = -lam
    softplus = jnp.maximum(neg, 0.0) + jnp.log(1.0 + jnp.exp(-jnp.abs(neg)))
    decay = -RG_C * softplus
    row = lax.broadcasted_iota(jnp.int32, (SUBLANES, c), 0)

    def group(g, hprev, base):
        r = pl.multiple_of(base + g * SUBLANES, SUBLANES)
        a = a_s[pl.ds(r, SUBLANES), :]
        u = u_s[pl.ds(r, SUBLANES), :]
        for d in (1, 2, 4):
            a_sh = pltpu.roll(a, d, 0)
            u_sh = pltpu.roll(u, d, 0)
            ok = row >= d
            u = jnp.where(ok, a * u_sh + u, u)
            a = jnp.where(ok, a * a_sh, a)
        h = u + a * hprev
        h_s[pl.ds(r, SUBLANES), :] = h
        return h[SUBLANES - 1:SUBLANES, :]

    hprev = h0_ref[...]
    for ci in range(s // chunk):
        r0 = ci * chunk
        xc = cb_ref[...]
        for j in range(CONV_WIDTH):
            xc = xc + xp_s[pad - keep + r0 + j:pad - keep + r0 + j + chunk, :] * cw_ref[j:j + 1, :]
        xcb = xc.astype(BF16)
        rgate = _sigmoid(jnp.dot(xcb, wa_ref[...], preferred_element_type=F32) + ba_ref[...])
        igate = _sigmoid(jnp.dot(xcb, wx_ref[...], preferred_element_type=F32) + bx_ref[...])
        log_a = decay * rgate
        e2 = jnp.exp(2.0 * log_a)
        a_s[r0:r0 + chunk, :] = jnp.exp(log_a)
        u_s[r0:r0 + chunk, :] = jnp.sqrt(jnp.tanh(-log_a) * (e2 + 1.0)) * (igate * xc)
        hprev = lax.fori_loop(0, chunk // SUBLANES, functools.partial(group, base=r0), hprev)
        y_ref[r0:r0 + chunk, :] = (h_s[r0:r0 + chunk, :] * _gelu_tanh(gate_ref[r0:r0 + chunk, :])).astype(y_ref.dtype)
    hlast_ref[...] = hprev


def _block_diag(w):
    n, c, _ = w.shape
    eye = jnp.eye(n, dtype=w.dtype)
    return (eye[:, None, :, None] * w[:, :, None, :]).reshape(n * c, n * c)


def _rg_lru(xr, gate, h0, conv_buf, conv_w, conv_b, wa_bd, ba, wx_bd, bx, lam):
    b, s, c = xr.shape
    chunk = min(RG_CHUNK, s)
    keep = CONV_WIDTH - 1
    full = lambda shape: pl.BlockSpec(shape, lambda bi: (0,) * len(shape))
    seq = pl.BlockSpec((None, s, c), lambda bi: (bi, 0, 0))
    out_shape = (jax.ShapeDtypeStruct((b, s, c), BF16),
                 jax.ShapeDtypeStruct((b, 1, c), F32),
                 jax.ShapeDtypeStruct((b, keep, c), F32))
    return pl.pallas_call(
        functools.partial(_rglru_kernel, chunk=chunk),
        grid=(b,),
        in_specs=[seq, seq,
                  pl.BlockSpec((None, 1, c), lambda bi: (bi, 0, 0)),
                  pl.BlockSpec((None, keep, c), lambda bi: (bi, 0, 0)),
                  full((CONV_WIDTH, c)), full((1, c)), full((c, c)), full((1, c)), full((c, c)), full((1, c)),
                  full((1, c))],
        out_specs=(seq, pl.BlockSpec((None, 1, c), lambda bi: (bi, 0, 0)),
                   pl.BlockSpec((None, keep, c), lambda bi: (bi, 0, 0))),
        out_shape=out_shape,
        scratch_shapes=[pltpu.VMEM((s + SUBLANES, c), F32), pltpu.VMEM((s, c), F32),
                        pltpu.VMEM((s, c), F32), pltpu.VMEM((s, c), F32)],
        compiler_params=_cparams(("parallel",)),
        name="rg_lru",
    )(xr, gate, h0, conv_buf, conv_w, conv_b, wa_bd, ba, wx_bd, bx, lam)


def _layer_norm(z, g, b):
    mu = jnp.mean(z, axis=1, keepdims=True)
    zc = z - mu
    var = jnp.mean(zc * zc, axis=1, keepdims=True)
    return zc * lax.rsqrt(var + LN_EPS) * g + b


def _outln_kernel(x_ref, o_ref, y_ref, wo_ref, g_ref, b_ref, x1_ref, x1t_ref, *, alpha):
    aw = o_ref.shape[1]
    mix = jnp.dot(o_ref[...], wo_ref[0:aw, :], preferred_element_type=F32)
    mix = mix + jnp.dot(y_ref[...], wo_ref[aw:, :], preferred_element_type=F32)
    x1 = _layer_norm(alpha * x_ref[...] + mix, g_ref[...], b_ref[...])
    x1_ref[...] = x1
    x1t_ref[...] = x1.T.astype(x1t_ref.dtype)


def _out_ln1(x, o, y, w_out_b, g, b, alpha, tm):
    t, d = x.shape
    row = lambda i: (i, 0)
    return pl.pallas_call(
        functools.partial(_outln_kernel, alpha=alpha),
        grid=(t // tm,),
        in_specs=[pl.BlockSpec((tm, d), row), pl.BlockSpec((tm, o.shape[1]), row),
                  pl.BlockSpec((tm, y.shape[1]), row),
                  pl.BlockSpec(w_out_b.shape, lambda i: (0, 0)),
                  pl.BlockSpec((1, d), lambda i: (0, 0)), pl.BlockSpec((1, d), lambda i: (0, 0))],
        out_specs=(pl.BlockSpec((tm, d), row), pl.BlockSpec((d, tm), lambda i: (0, i))),
        out_shape=(jax.ShapeDtypeStruct((t, d), F32), jax.ShapeDtypeStruct((d, t), BF16)),
        compiler_params=_cparams(("parallel",)),
        name="out_ln1",
    )(x, o, y, w_out_b, g, b)


def _pair_list(limit):
    return [(i, j) for i in range(limit) for j in range(limit) if (i + 1) * (j + 1) <= limit]


def _route_kernel(x1t_ref, wqt_ref, keys_ref, s2_ref, e2_ref, thr_ref, e1z_ref, sc_s, top_s):
    tt = x1t_ref.shape[1]
    n_top = top_s.shape[1]
    qt = jnp.dot(wqt_ref[...], x1t_ref[...], preferred_element_type=F32)
    for hp in range(PEER_HEADS * 2):
        qhp = qt[hp * PEER_QHALF:(hp + 1) * PEER_QHALF, :].astype(BF16)
        sc = jnp.dot(keys_ref[hp], qhp, preferred_element_type=F32)
        sc_s[hp] = sc
        h, p = divmod(hp, 2)
        rem = sc
        for k in range(n_top):
            m = jnp.max(rem, axis=0, keepdims=True)
            top_s[p, k, h:h + 1, :] = m
            rem = jnp.where(rem == m, -jnp.inf, rem)

    a = [top_s[0, k] for k in range(n_top)]
    b = [top_s[1, k] for k in range(n_top)]
    pairs = _pair_list(n_top)
    cand = jnp.stack([a[i] + b[j] for i, j in pairs], axis=0)
    inf = jnp.full(cand.shape[1:], jnp.inf, F32)
    t16 = inf
    t17 = inf
    for ci in range(len(pairs)):
        cv = cand[ci]
        rank = jnp.sum(jnp.where(cand > cv[None], 1.0, 0.0), axis=0)
        t16 = jnp.minimum(t16, jnp.where(rank <= PEER_TOPK - 1, cv, inf))
        t17 = jnp.minimum(t17, jnp.where(rank <= PEER_TOPK, cv, inf))
    tau = 0.5 * (t16 + t17)
    mx = a[0] + b[0]
    z = jnp.sum(jnp.where(cand >= tau[None], jnp.exp(cand - mx[None]), 0.0), axis=0)
    zinv = 1.0 / z
    for h in range(PEER_HEADS):
        s1 = sc_s[2 * h]
        s2 = sc_s[2 * h + 1]
        s2_ref[h] = s2
        e2_ref[h] = jnp.exp(s2 - b[0][h:h + 1, :])
        thr_ref[h] = tau[h:h + 1, :] - s1
        e1z_ref[h] = jnp.exp(s1 - a[0][h:h + 1, :]) * zinv[h:h + 1, :]


def _peer_route(x1t, wqt_b, keys_b, tt):
    d, t = x1t.shape
    nq = wqt_b.shape[0]
    out = jax.ShapeDtypeStruct((PEER_HEADS, PEER_KEYS, t), F32)
    blk = pl.BlockSpec((PEER_HEADS, PEER_KEYS, tt), lambda i: (0, 0, i))
    return pl.pallas_call(
        _route_kernel,
        grid=(t // tt,),
        in_specs=[pl.BlockSpec((d, tt), lambda i: (0, i)),
                  pl.BlockSpec((nq, d), lambda i: (0, 0)),
                  pl.BlockSpec(keys_b.shape, lambda i: (0, 0, 0))],
        out_specs=(blk, blk, blk, blk),
        out_shape=(out, out, out, out),
        scratch_shapes=[pltpu.VMEM((PEER_HEADS * 2, PEER_KEYS, tt), F32),
                        pltpu.VMEM((2, PEER_TOPK + 1, PEER_HEADS, tt), F32)],
        compiler_params=_cparams(("parallel",)),
        name="peer_route",
    )(x1t, wqt_b, keys_b)


def _dense_kernel(x1t_ref, u_ref, vt_ref, s2_ref, e2_ref, thr_ref, e1z_ref, out_ref, h_s, w_s, acc_s, thr_s, e1z_s):
    c = pl.program_id(1)
    tm = x1t_ref.shape[1]

    @pl.when(c == 0)
    def _():
        acc_s[...] = jnp.zeros(acc_s.shape, F32)

    h_s[...] = jnp.dot(u_ref[...], x1t_ref[...], preferred_element_type=F32)

    for i1 in range(DENSE_I1):
        for h in range(PEER_HEADS):
            thr_s[i1, h:h + 1, :] = thr_ref[h, i1:i1 + 1, :]
            e1z_s[i1, h:h + 1, :] = e1z_ref[h, i1:i1 + 1, :]

    def first_key(i1, carry):
        base = pl.multiple_of(i1 * PEER_KEYS, PEER_KEYS)
        for lt in range(tm // LANES):
            lanes = slice(lt * LANES, (lt + 1) * LANES)
            for rb in range(PEER_KEYS // DENSE_ROWS):
                rows = slice(rb * DENSE_ROWS, (rb + 1) * DENSE_ROWS)
                gate = jnp.zeros((DENSE_ROWS, LANES), F32)
                for h in range(PEER_HEADS):
                    thr = thr_s[i1, h:h + 1, lanes]
                    e1z = e1z_s[i1, h:h + 1, lanes]
                    sel = jnp.where(s2_ref[h, rows, lanes] >= thr, e2_ref[h, rows, lanes], 0.0)
                    gate = gate + sel * e1z
                hrows = pl.ds(base + rb * DENSE_ROWS, DENSE_ROWS)
                w_s[hrows, lanes] = (gate * _gelu_tanh(h_s[hrows, lanes])).astype(w_s.dtype)
        return carry

    lax.fori_loop(0, DENSE_I1, first_key, 0)
    acc_s[...] += jnp.dot(vt_ref[...], w_s[...], preferred_element_type=F32)

    @pl.when(c == pl.num_programs(1) - 1)
    def _():
        out_ref[...] = acc_s[...].T


def _peer_dense(x1t, u_b, vt_b, s2, e2, thr, e1z, tm):
    d, t = x1t.shape
    n_exp = u_b.shape[0]
    ec = DENSE_I1 * PEER_KEYS
    res = pl.BlockSpec((PEER_HEADS, PEER_KEYS, tm), lambda i, c: (0, 0, i))
    per = pl.BlockSpec((PEER_HEADS, DENSE_I1, tm), lambda i, c: (0, c, i))
    return pl.pallas_call(
        _dense_kernel,
        grid=(t // tm, n_exp // ec),
        in_specs=[pl.BlockSpec((d, tm), lambda i, c: (0, i)),
                  pl.BlockSpec((ec, d), lambda i, c: (c, 0)),
                  pl.BlockSpec((d, ec), lambda i, c: (0, c)),
                  res, res, per, per],
        out_specs=pl.BlockSpec((tm, d), lambda i, c: (i, 0)),
        out_shape=jax.ShapeDtypeStruct((t, d), F32),
        scratch_shapes=[pltpu.VMEM((ec, tm), F32), pltpu.VMEM((ec, tm), BF16), pltpu.VMEM((d, tm), F32),
                        pltpu.VMEM((DENSE_I1, PEER_HEADS, tm), F32), pltpu.VMEM((DENSE_I1, PEER_HEADS, tm), F32)],
        compiler_params=_cparams(("parallel", "arbitrary")),
        name="peer_dense",
    )(x1t, u_b, vt_b, s2, e2, thr, e1z)


def _ln2_kernel(x1_ref, ffn_ref, pe_ref, g_ref, b_ref, gw_ref, gb_ref, pw_ref, y_ref, *, alpha):
    x2 = _layer_norm(alpha * x1_ref[...] + ffn_ref[...], g_ref[...], b_ref[...])
    gate = _sigmoid(jnp.dot(x2.astype(BF16), gw_ref[...], preferred_element_type=F32) + gb_ref[...])
    pe = jnp.dot(pe_ref[...].astype(BF16), pw_ref[...], preferred_element_type=F32)
    y_ref[...] = x2 + gate * pe


def _ln2_ple(x1, ffn, pe, g, b, gate_w_b, gate_b, ple_w_b, alpha, tm):
    t, d = x1.shape
    pd = pe.shape[1]
    row = lambda i: (i, 0)
    one = lambda i: (0, 0)
    return pl.pallas_call(
        functools.partial(_ln2_kernel, alpha=alpha),
        grid=(t // tm,),
        in_specs=[pl.BlockSpec((tm, d), row), pl.BlockSpec((tm, d), row), pl.BlockSpec((tm, pd), row),
                  pl.BlockSpec((1, d), one), pl.BlockSpec((1, d), one),
                  pl.BlockSpec((d, d), one), pl.BlockSpec((1, d), one), pl.BlockSpec((pd, d), one)],
        out_specs=pl.BlockSpec((tm, d), row),
        out_shape=jax.ShapeDtypeStruct((t, d), F32),
        compiler_params=_cparams(("parallel",)),
        name="ln2_ple",
    )(x1, ffn, pe, g, b, gate_w_b, gate_b, ple_w_b)


def _token_mixer_tail(x, o, yr, pe, w, alpha):
    t = x.shape[0]
    tm = min(TOK_TM, t)
    x1, x1t = _out_ln1(x, o, yr, w["w_out"], w["ln1_g"], w["ln1_b"], alpha, tm)
    s2, e2, thr, e1z = _peer_route(x1t, w["peer_wqt"], w["peer_keys"], min(ROUTE_TT, t))
    ffn = _peer_dense(x1t, w["peer_u"], w["peer_vt"], s2, e2, thr, e1z, min(DENSE_TM, t))
    return _ln2_ple(x1, ffn, pe, w["ln2_g"], w["ln2_b"], w["ple_gate_w"], w["ple_gate_b"], w["ple_w"], alpha, tm)


def kernel(x_prompt, x_sample, p_prompt, p_sample, cache_k, cache_v, state_h, state_conv, page_table, w_in, lambda_q1, lambda_k1, lambda_q2, lambda_k2, subln_g, conv_w, conv_b, rg_wa, rg_ba, rg_wx, rg_bx, rg_lambda, w_out, ln1_g, ln1_b, peer_wq, peer_keys, peer_u, peer_v, ln2_g, ln2_b, ple_w, ple_gate_w, ple_gate_b):
    depth = w_in.shape[0]
    bp, s, d = x_prompt.shape
    bd, t, _ = x_sample.shape
    n_pool, page_size = cache_k.shape[1], cache_k.shape[2]
    past = page_table.shape[1] * page_size
    rgw = rg_lambda.shape[1]
    alpha = (2.0 * depth) ** 0.25

    tab_p = _rotary_tables(jnp.arange(s))
    tab_d = tuple(jnp.tile(tb, (bd, 1)) for tb in _rotary_tables(past + jnp.arange(t)))
    cache_k2 = cache_k.reshape(depth * n_pool, page_size, ATTN_HEADS * 2 * QK_DIM)
    cache_v2 = cache_v.reshape(depth * n_pool, page_size, ATTN_WIDTH)

    xp, xd = x_prompt, x_sample
    outs = {n: [] for n in ("kp", "vp", "hp", "cp", "kd", "vd", "hd", "cd")}
    row = lambda v: v.reshape(1, -1)
    for l in range(depth):
        lam_init = 0.8 - 0.6 * math.exp(-0.3 * l)
        lam_vecs = (row(lambda_q1[l]), row(lambda_k1[l]), row(lambda_q2[l]), row(lambda_k2[l]))
        g = row(subln_g[l])
        w = {
            "w_out": w_out[l].astype(BF16), "ln1_g": row(ln1_g[l]), "ln1_b": row(ln1_b[l]),
            "peer_wqt": peer_wq[l].T.astype(BF16),
            "peer_keys": peer_keys[l].reshape(PEER_HEADS * 2, PEER_KEYS, PEER_QHALF).astype(BF16),
            "peer_u": peer_u[l].astype(BF16), "peer_vt": peer_v[l].T.astype(BF16),
            "ln2_g": row(ln2_g[l]), "ln2_b": row(ln2_b[l]),
            "ple_gate_w": ple_gate_w[l].astype(BF16), "ple_gate_b": row(ple_gate_b[l]),
            "ple_w": ple_w[l].astype(BF16),
        }
        w_in_b = w_in[l].astype(BF16)
        wa_bd = _block_diag(rg_wa[l]).astype(BF16)
        wx_bd = _block_diag(rg_wx[l]).astype(BF16)
        rg_args = (conv_w[l], row(conv_b[l]), wa_bd, row(rg_ba[l]), wx_bd, row(rg_bx[l]), row(rg_lambda[l]))

        q, kb, vb, k, v, xr, gate = _project(xp, w_in_b, tab_p, min(PROJ_TM, s))
        o = _prompt_attention(q, kb, vb, lam_vecs, g, lam_init)
        yr, h_last, conv_new = _rg_lru(xr, gate, jnp.zeros((bp, 1, rgw), F32),
                                       jnp.zeros((bp, CONV_WIDTH - 1, rgw), F32), *rg_args)
        y = _token_mixer_tail(xp.reshape(bp * s, d), o.reshape(bp * s, -1), yr.reshape(bp * s, -1),
                              p_prompt[l].reshape(bp * s, -1), w, alpha)
        xp = y.reshape(bp, s, d)
        outs["kp"].append(k.reshape(bp, s, ATTN_HEADS, 2 * QK_DIM))
        outs["vp"].append(v.reshape(bp, s, ATTN_HEADS, V_DIM))
        outs["hp"].append(h_last.reshape(bp, rgw))
        outs["cp"].append(conv_new)

        q, kb, vb, k, v, xr, gate = _project(xd.reshape(1, bd * t, d), w_in_b, tab_d, bd * t)
        unflat = lambda z: z.reshape(bd, t, z.shape[-1])
        o = _sample_attention(unflat(q), unflat(kb), unflat(vb), cache_k2, cache_v2, page_table, l * n_pool,
                              lam_vecs, g, lam_init)
        yr, h_last, conv_new = _rg_lru(unflat(xr), unflat(gate), state_h[l].reshape(bd, 1, rgw),
                                       state_conv[l], *rg_args)
        y = _token_mixer_tail(xd.reshape(bd * t, d), o.reshape(bd * t, -1), yr.reshape(bd * t, -1),
                              p_sample[l].reshape(bd * t, -1), w, alpha)
        xd = y.reshape(bd, t, d)
        outs["kd"].append(k.reshape(bd, t, ATTN_HEADS, 2 * QK_DIM))
        outs["vd"].append(v.reshape(bd, t, ATTN_HEADS, V_DIM))
        outs["hd"].append(h_last.reshape(bd, rgw))
        outs["cd"].append(conv_new)

    st = lambda n: jnp.stack(outs[n])
    return (xp, xd, st("kp"), st("vp"), st("hp"), st("cp"), st("kd"), st("vd"), st("hd"), st("cd"))
```
